```python
import math
import jax, jax.numpy as jnp
from jax import lax
import numpy as np

D_MODEL = 1024
BATCH = 4
SEQ = 8192
DEPTH = 2

N_A_LAYERS = DEPTH // 2
N_B_LAYERS = DEPTH - N_A_LAYERS

SSM_D_INNER = 2 * D_MODEL
SSM_HEAD_DIM = 64
SSM_N_HEADS = SSM_D_INNER // SSM_HEAD_DIM
SSM_D_STATE = 128
SSM_N_GROUPS = 8
SSM_HEADS_PER_GROUP = SSM_N_HEADS // SSM_N_GROUPS
SSM_CONV = 4
SSM_CHUNK = 128
SSM_CONV_DIM = SSM_D_INNER + 2 * SSM_N_GROUPS * SSM_D_STATE
SSM_IN_DIM = SSM_D_INNER + SSM_CONV_DIM + SSM_N_HEADS

SB_N_HEADS = 16
SB_HEAD_DIM = D_MODEL // SB_N_HEADS
SB_WIDTH = SB_N_HEADS * SB_HEAD_DIM
SB_BLOCK = 128

FFN_DIM = ((8 * D_MODEL // 3 + 255) // 256) * 256
FFN_CONV = 3

NORM_EPS = 1e-6

kernel_name = "yoco_mamba2_stickbreaking_convffn_adaln"


def rmsnorm(x, g):
    xf = x.astype(jnp.float32)
    y = xf * lax.rsqrt(jnp.mean(xf * xf, axis=-1, keepdims=True) + NORM_EPS)
    return (y * g.astype(jnp.float32)).astype(x.dtype)


def modulate(h, shift, scale):
    return h * (1.0 + scale[:, None, :]) + shift[:, None, :]


def causal_dwconv(x, w, b):
    width = w.shape[0]
    s = x.shape[1]
    xp = jnp.pad(x, ((0, 0), (width - 1, 0), (0, 0)))
    out = xp[:, 0:s] * w[0]
    for k in range(1, width):
        out = out + xp[:, k:k + s] * w[k]
    return out + b


def ssd_chunked(x, dt, a_neg, bm, cm):
    b, l = x.shape[0], x.shape[1]
    nc = l // SSM_CHUNK
    G, R, P, N, Q = SSM_N_GROUPS, SSM_HEADS_PER_GROUP, SSM_HEAD_DIM, SSM_D_STATE, SSM_CHUNK
    xdt = (x * dt[..., None].astype(x.dtype)).reshape(b, nc, Q, G, R, P)
    bc = bm.reshape(b, nc, Q, G, N)
    cc = cm.reshape(b, nc, Q, G, N)
    a = (dt * a_neg).reshape(b, nc, Q, G, R)
    acs = jnp.moveaxis(jnp.cumsum(a, axis=2), 2, -1)
    causal = jnp.tril(jnp.ones((Q, Q), dtype=bool))
    seg = jnp.exp(jnp.where(causal, acs[..., :, None] - acs[..., None, :], -jnp.inf)).astype(x.dtype)
    cb = jnp.einsum('bclgn,bcsgn->bcgls', cc, bc)
    y_diag = jnp.einsum('bcgrls,bcsgrp->bclgrp', cb[:, :, :, None] * seg, xdt)
    decay_states = jnp.exp(acs[..., -1:] - acs).astype(x.dtype)
    states = jnp.einsum('bcsgn,bcgrs,bcsgrp->bcgrpn', bc, decay_states, xdt)
    chunk_decay = jnp.exp(acs[..., -1])

    def step(h, inp):
        st, dec = inp
        return h * dec[..., None, None] + st, h

    h0 = jnp.zeros((b, G, R, P, N), jnp.float32)
    _, prev = lax.scan(step, h0, (jnp.moveaxis(states, 1, 0).astype(jnp.float32),
                                  jnp.moveaxis(chunk_decay, 1, 0)))
    prev = jnp.moveaxis(prev, 0, 1).astype(x.dtype)
    y_off = jnp.einsum('bclgn,bcgrpn,bcgrl->bclgrp', cc, prev, jnp.exp(acs).astype(x.dtype))
    return (y_diag + y_off).reshape(b, l, G * R, P)


def mamba2_mixer(h, w_in, conv_w, conv_b, dt_bias, a_log, d_skip, norm_g, w_out):
    b, l, _ = h.shape
    zxbcdt = h @ w_in
    z = zxbcdt[..., :SSM_D_INNER]
    xbc = zxbcdt[..., SSM_D_INNER:SSM_D_INNER + SSM_CONV_DIM]
    dt_raw = zxbcdt[..., SSM_D_INNER + SSM_CONV_DIM:]
    xbc = jax.nn.silu(causal_dwconv(xbc, conv_w, conv_b))
    gn = SSM_N_GROUPS * SSM_D_STATE
    xs = xbc[..., :SSM_D_INNER].reshape(b, l, SSM_N_HEADS, SSM_HEAD_DIM)
    bm = xbc[..., SSM_D_INNER:SSM_D_INNER + gn].reshape(b, l, SSM_N_GROUPS, SSM_D_STATE)
    cm = xbc[..., SSM_D_INNER + gn:].reshape(b, l, SSM_N_GROUPS, SSM_D_STATE)
    dt = jax.nn.softplus(dt_raw.astype(jnp.float32) + dt_bias.astype(jnp.float32))
    a_neg = -jnp.exp(a_log.astype(jnp.float32))
    y = ssd_chunked(xs, dt, a_neg, bm, cm) + xs * d_skip[:, None]
    y = y.reshape(b, l, SSM_D_INNER)
    y = rmsnorm(y * jax.nn.silu(z), norm_g)
    return y @ w_out


def stick_breaking_attention(q, k, v):
    b, nh, s, d = q.shape
    nb = s // SB_BLOCK
    scale = 1.0 / math.sqrt(d)
    pos = jnp.arange(SB_BLOCK)

    def one_block(qi):
        q_blk = lax.dynamic_slice_in_dim(q, qi * SB_BLOCK, SB_BLOCK, axis=2)
        qpos = qi * SB_BLOCK + pos

        def body(step, carry):
            acc, suffix = carry
            kb = qi - step
            k_blk = lax.dynamic_slice_in_dim(k, kb * SB_BLOCK, SB_BLOCK, axis=2)
            v_blk = lax.dynamic_slice_in_dim(v, kb * SB_BLOCK, SB_BLOCK, axis=2)
            kpos = kb * SB_BLOCK + pos
            live = kpos[None, :] < qpos[:, None]
            z = jnp.einsum('bhtd,bhsd->bhts', q_blk, k_blk).astype(jnp.float32) * scale
            log_1mb = jnp.where(live, jax.nn.log_sigmoid(-z), 0.0)
            row_total = jnp.sum(log_1mb, axis=-1)
            later = row_total[..., None] - jnp.cumsum(log_1mb, axis=-1)
            log_w = jax.nn.log_sigmoid(z) + later + suffix[..., None]
            w = jnp.where(live, jnp.exp(log_w), 0.0)
            acc = acc + jnp.einsum('bhts,bhsd->bhtd', w.astype(v.dtype), v_blk).astype(jnp.float32)
            return acc, suffix + row_total

        init = (jnp.zeros((b, nh, SB_BLOCK, d), jnp.float32), jnp.zeros((b, nh, SB_BLOCK), jnp.float32))
        acc, _ = lax.fori_loop(0, qi + 1, body, init)
        return acc.astype(q.dtype)

    out = lax.map(one_block, jnp.arange(nb))
    return jnp.transpose(out, (1, 2, 0, 3, 4)).reshape(b, nh, s, d)


def split_heads(t):
    b, s, _ = t.shape
    return t.reshape(b, s, SB_N_HEADS, SB_HEAD_DIM).transpose(0, 2, 1, 3)


def sb_mixer(h, k, v, w_q, w_o):
    b, s, _ = h.shape
    q = split_heads(h @ w_q)
    o = stick_breaking_attention(q, k, v)
    return o.transpose(0, 2, 1, 3).reshape(b, s, SB_WIDTH) @ w_o


def conv_ffn(h, w_in, conv_w, conv_b, w_out):
    u = causal_dwconv(h @ w_in, conv_w, conv_b)
    gate, val = u[..., :FFN_DIM], u[..., FFN_DIM:]
    return (jax.nn.silu(gate) * val) @ w_out


def setup_inputs(seed: int = 0) -> dict:
    key = jax.random.key(seed)
    ks = iter(jax.random.split(key, 40))
    f32 = jnp.float32
    D = D_MODEL

    def nrm(shape, scale):
        return jax.random.normal(next(ks), shape, f32) * scale

    def gain(shape):
        return 1.0 + 0.02 * jax.random.normal(next(ks), shape, f32)

    x = jax.random.normal(next(ks), (BATCH, SEQ, D), f32)
    c = jax.random.normal(next(ks), (BATCH, D), f32)
    ada_w = nrm((DEPTH, D, 6 * D), 0.5 * D ** -0.5)
    ada_b = nrm((DEPTH, 6 * D), 0.02)
    norm_mix_g = gain((DEPTH, D))
    norm_ffn_g = gain((DEPTH, D))
    ssm_w_in = nrm((N_A_LAYERS, D, SSM_IN_DIM), D ** -0.5)
    ssm_conv_w = nrm((N_A_LAYERS, SSM_CONV, SSM_CONV_DIM), SSM_CONV ** -0.5)
    ssm_conv_b = nrm((N_A_LAYERS, SSM_CONV_DIM), 0.02)
    u = jax.random.uniform(next(ks), (N_A_LAYERS, SSM_N_HEADS), f32)
    dt0 = jnp.exp(u * (math.log(0.1) - math.log(0.001)) + math.log(0.001))
    ssm_dt_bias = dt0 + jnp.log(-jnp.expm1(-dt0))
    ssm_a_log = jnp.log(jax.random.uniform(next(ks), (N_A_LAYERS, SSM_N_HEADS), f32, 1.0, 16.0))
    ssm_d = gain((N_A_LAYERS, SSM_N_HEADS))
    ssm_norm_g = gain((N_A_LAYERS, SSM_D_INNER))
    ssm_w_out = nrm((N_A_LAYERS, SSM_D_INNER, D), SSM_D_INNER ** -0.5)
    kv_ada_w = nrm((D, 2 * D), 0.5 * D ** -0.5)
    kv_ada_b = nrm((2 * D,), 0.02)
    kv_norm_g = gain((D,))
    w_k = nrm((D, SB_WIDTH), D ** -0.5)
    w_v = nrm((D, SB_WIDTH), D ** -0.5)
    sb_w_q = nrm((N_B_LAYERS, D, SB_WIDTH), D ** -0.5)
    sb_w_o = nrm((N_B_LAYERS, SB_WIDTH, D), SB_WIDTH ** -0.5)
    ffn_w_in = nrm((DEPTH, D, 2 * FFN_DIM), D ** -0.5)
    ffn_conv_w = nrm((DEPTH, FFN_CONV, 2 * FFN_DIM), FFN_CONV ** -0.5)
    ffn_conv_b = nrm((DEPTH, 2 * FFN_DIM), 0.02)
    ffn_w_out = nrm((DEPTH, FFN_DIM, D), FFN_DIM ** -0.5)
    final_ada_w = nrm((D, 2 * D), 0.5 * D ** -0.5)
    final_ada_b = nrm((2 * D,), 0.02)
    final_norm_g = gain((D,))
    return {"x": x, "c": c, "ada_w": ada_w, "ada_b": ada_b,
            "norm_mix_g": norm_mix_g, "norm_ffn_g": norm_ffn_g,
            "ssm_w_in": ssm_w_in, "ssm_conv_w": ssm_conv_w, "ssm_conv_b": ssm_conv_b,
            "ssm_dt_bias": ssm_dt_bias, "ssm_a_log": ssm_a_log, "ssm_d": ssm_d,
            "ssm_norm_g": ssm_norm_g, "ssm_w_out": ssm_w_out,
            "kv_ada_w": kv_ada_w, "kv_ada_b": kv_ada_b, "kv_norm_g": kv_norm_g,
            "w_k": w_k, "w_v": w_v, "sb_w_q": sb_w_q, "sb_w_o": sb_w_o,
            "ffn_w_in": ffn_w_in, "ffn_conv_w": ffn_conv_w, "ffn_conv_b": ffn_conv_b,
            "ffn_w_out": ffn_w_out, "final_ada_w": final_ada_w, "final_ada_b": final_ada_b,
            "final_norm_g": final_norm_g}


def reference(x, c, ada_w, ada_b, norm_mix_g, norm_ffn_g,
              ssm_w_in, ssm_conv_w, ssm_conv_b, ssm_dt_bias, ssm_a_log, ssm_d,
              ssm_norm_g, ssm_w_out, kv_ada_w, kv_ada_b, kv_norm_g, w_k, w_v,
              sb_w_q, sb_w_o, ffn_w_in, ffn_conv_w, ffn_conv_b, ffn_w_out,
              final_ada_w, final_ada_b, final_norm_g):
    c_act = jax.nn.silu(c)
    k = v = None
    for layer in range(DEPTH):
        mod = c_act @ ada_w[layer] + ada_b[layer]
        sh1, sc1, g1, sh2, sc2, g2 = jnp.split(mod, 6, axis=-1)
        h = modulate(rmsnorm(x, norm_mix_g[layer]), sh1, sc1)
        if layer < N_A_LAYERS:
            y = mamba2_mixer(h, ssm_w_in[layer], ssm_conv_w[layer], ssm_conv_b[layer],
                             ssm_dt_bias[layer], ssm_a_log[layer], ssm_d[layer],
                             ssm_norm_g[layer], ssm_w_out[layer])
        else:
            j = layer - N_A_LAYERS
            y = sb_mixer(h, k, v, sb_w_q[j], sb_w_o[j])
        x = x + g1[:, None, :] * y
        h = modulate(rmsnorm(x, norm_ffn_g[layer]), sh2, sc2)
        x = x + g2[:, None, :] * conv_ffn(h, ffn_w_in[layer], ffn_conv_w[layer],
                                          ffn_conv_b[layer], ffn_w_out[layer])
        if layer == N_A_LAYERS - 1:
            kv_mod = c_act @ kv_ada_w + kv_ada_b
            kv_sh, kv_sc = jnp.split(kv_mod, 2, axis=-1)
            hk = modulate(rmsnorm(x, kv_norm_g), kv_sh, kv_sc)
            k = split_heads(hk @ w_k)
            v = split_heads(hk @ w_v)
    f_mod = c_act @ final_ada_w + final_ada_b
    f_sh, f_sc = jnp.split(f_mod, 2, axis=-1)
    return modulate(rmsnorm(x, final_norm_g), f_sh, f_sc)
```

```python
import functools
import math

import jax
import jax.numpy as jnp
from jax import lax
from jax.experimental import pallas as pl
from jax.experimental.pallas import tpu as pltpu

F32 = jnp.float32
BF16 = jnp.bfloat16

D_MODEL = 1024
SSM_D_INNER = 2 * D_MODEL
SSM_HEAD_DIM = 64
SSM_N_HEADS = SSM_D_INNER // SSM_HEAD_DIM
SSM_D_STATE = 128
SSM_N_GROUPS = 8
SSM_HEADS_PER_GROUP = SSM_N_HEADS // SSM_N_GROUPS
SSM_CONV = 4
SSM_CHUNK = 128
SSM_GN = SSM_N_GROUPS * SSM_D_STATE
SSM_CONV_DIM = SSM_D_INNER + 2 * SSM_GN
SB_N_HEADS = 16
SB_HEAD_DIM = D_MODEL // SB_N_HEADS
SB_BLOCK = 128
FFN_DIM = ((8 * D_MODEL // 3 + 255) // 256) * 256
FFN_CONV = 3
NORM_EPS = 1e-6

LANES = 128
SUBLANES = 8
VMEM_LIMIT = 56 * 1024 * 1024
TOKEN_TILE = 512
FFN_COL_CHUNK = 256
SSM_COL_CHUNK = 512
SB_LOG_ZERO = -110.0


def _dot(a, b):
    return jnp.dot(a, b, preferred_element_type=F32)


def _dot_nt(a, b):
    return lax.dot_general(a, b, (((1,), (1,)), ((), ())), preferred_element_type=F32)


def _split2(x):
    hi = x.astype(BF16)
    lo = (x - hi.astype(F32)).astype(BF16)
    return hi, lo


def _split3(x):
    hi = x.astype(BF16)
    r = x - hi.astype(F32)
    mid = r.astype(BF16)
    lo = (r - mid.astype(F32)).astype(BF16)
    return hi, mid, lo


def _sigmoid(x):
    return 1.0 / (1.0 + jnp.exp(-x))


def _silu(x):
    return x * _sigmoid(x)


def _softplus(x):
    return jnp.maximum(x, 0.0) + jnp.log(1.0 + jnp.exp(-jnp.abs(x)))


def _rms_mod(x, g, sh, sc):
    r = lax.rsqrt(jnp.mean(x * x, axis=-1, keepdims=True) + NORM_EPS)
    return (x * r * g) * (1.0 + sc) + sh


def _const_spec(shape):
    n = len(shape)
    return pl.BlockSpec(shape, lambda *_: (0,) * n, pipeline_mode=pl.Buffered(1))


def _params(n_axes):
    return pltpu.CompilerParams(dimension_semantics=("arbitrary",) * n_axes,
                                vmem_limit_bytes=VMEM_LIMIT)


def _mods_kernel(c_ref, w_ref, b_ref, o_ref):
    ca = _silu(c_ref[...])
    c_hi, c_lo = _split2(ca)
    w_hi, w_lo = _split2(w_ref[...])
    o_ref[...] = _dot(c_hi, w_hi) + _dot(c_hi, w_lo) + _dot(c_lo, w_hi) + b_ref[...]


def _mods(c_pad, w, b, tn=1024):
    nl, d, n = w.shape
    return pl.pallas_call(
        _mods_kernel,
        grid=(nl, n // tn),
        in_specs=[pl.BlockSpec((SUBLANES, d), lambda l, j: (0, 0)),
                  pl.BlockSpec((None, d, tn), lambda l, j: (l, 0, j)),
                  pl.BlockSpec((None, 1, tn), lambda l, j: (l, 0, j))],
        out_specs=pl.BlockSpec((None, SUBLANES, tn), lambda l, j: (l, 0, j)),
        out_shape=jax.ShapeDtypeStruct((nl, SUBLANES, n), F32),
        compiler_params=_params(2),
        name="mods",
    )(c_pad, w, b)


def _causal_conv(u, wbuf, carry, cols, cw, cb, first):
    tm = u.shape[0]
    width = cw.shape[0]

    @pl.when(first)
    def _():
        carry[:, cols] = jnp.zeros((SUBLANES, u.shape[1]), F32)

    wbuf[0:SUBLANES, :] = carry[:, cols]
    wbuf[SUBLANES:SUBLANES + tm, :] = u
    carry[:, cols] = wbuf[tm:tm + SUBLANES, :]
    acc = cb + cw[width - 1:width, :] * u
    for k in range(width - 1):
        off = SUBLANES - (width - 1) + k
        acc = acc + cw[k:k + 1, :] * wbuf[off:off + tm, :]
    return acc


def _ssm_in_kernel(x_ref, g_ref, sh_ref, sc_ref, wz_ref, wx_ref, wdt_hi_ref, wdt_lo_ref,
                   cw_ref, cb_ref, dtb_ref, z_ref, xbc_ref, dt_ref, wbuf, carry):
    first = pl.program_id(1) == 0
    hf = _rms_mod(x_ref[...], g_ref[...], sh_ref[...], sc_ref[...])
    h_hi, h_lo = _split2(hf)
    z_ref[...] = _dot(h_hi, wz_ref[...]).astype(BF16)
    dt_raw = (_dot(h_hi, wdt_hi_ref[...]) + _dot(h_hi, wdt_lo_ref[...])
              + _dot(h_lo, wdt_hi_ref[...]))
    dt_ref[...] = _softplus(dt_raw + dtb_ref[...])
    for c in range(SSM_CONV_DIM // SSM_COL_CHUNK):
        cols = slice(c * SSM_COL_CHUNK, (c + 1) * SSM_COL_CHUNK)
        u = _dot(h_hi, wx_ref[:, cols])
        v = _causal_conv(u, wbuf, carry, cols, cw_ref[:, cols], cb_ref[:, cols], first)
        xbc_ref[:, cols] = _silu(v).astype(BF16)


def _ssm_in(x, g, sh, sc, wz, wx, wdt_hi, wdt_lo, cw, cb, dtb, tm):
    bsz, s, d = x.shape
    row = lambda n: pl.BlockSpec((None, tm, n), lambda b, j: (b, j, 0))
    vec = lambda n: pl.BlockSpec((None, 1, n), lambda b, j: (b, 0, 0))
    return pl.pallas_call(
        _ssm_in_kernel,
        grid=(bsz, s // tm),
        in_specs=[row(d), _const_spec((1, d)), vec(d), vec(d),
                  _const_spec(wz.shape), _const_spec(wx.shape),
                  _const_spec(wdt_hi.shape), _const_spec(wdt_lo.shape),
                  _const_spec(cw.shape), _const_spec(cb.shape), _const_spec(dtb.shape)],
        out_specs=[row(SSM_D_INNER), row(SSM_CONV_DIM), row(LANES)],
        out_shape=[jax.ShapeDtypeStruct((bsz, s, SSM_D_INNER), BF16),
                   jax.ShapeDtypeStruct((bsz, s, SSM_CONV_DIM), BF16),
                   jax.ShapeDtypeStruct((bsz, s, LANES), F32)],
        scratch_shapes=[pltpu.VMEM((tm + SUBLANES, SSM_COL_CHUNK), F32),
                        pltpu.VMEM((SUBLANES, SSM_CONV_DIM), F32)],
        compiler_params=_params(2),
        name="ssm_in",
    )(x, g, sh, sc, wz, wx, wdt_hi, wdt_lo, cw, cb, dtb)


def _ssd_kernel(z_ref, xs_ref, b_ref, c_ref, dt_ref, x_ref, g1_ref, alog_ref, dskip_ref,
                ng_ref, wout_ref, xo_ref, state_ref, y_ref, *, tm):
    q = SSM_CHUNK

    @pl.when(pl.program_id(1) == 0)
    def _():
        state_ref[...] = jnp.zeros(state_ref.shape, F32)

    a_neg = -jnp.exp(alog_ref[...])
    ri = lax.broadcasted_iota(jnp.int32, (q, q), 0)
    ci = lax.broadcasted_iota(jnp.int32, (q, q), 1)
    causal = ri >= ci
    tril = jnp.where(causal, 1.0, 0.0).astype(BF16)
    first_head = ci < SSM_HEAD_DIM

    def chunk(ck, carry_unused):
        rows = pl.ds(pl.multiple_of(ck * q, q), q)
        dtc = dt_ref[rows, :]
        a3 = jnp.concatenate(_split3(dtc * a_neg), axis=1)
        r = _dot(tril, a3)
        acs = r[:, 0:LANES] + r[:, LANES:2 * LANES] + r[:, 2 * LANES:3 * LANES]
        acs_t = acs.T
        dt_t = dtc.T
        for grp in range(SSM_N_GROUPS):
            gcols = slice(grp * SSM_D_STATE, (grp + 1) * SSM_D_STATE)
            bg = b_ref[rows, gcols]
            cg = c_ref[rows, gcols]
            cb = _dot_nt(cg, bg)
            bg_t = bg.astype(F32).T
            cg_f = cg.astype(F32)
            for pr in range(SSM_HEADS_PER_GROUP // 2):
                h0 = grp * SSM_HEADS_PER_GROUP + 2 * pr
                pcols = slice(h0 * SSM_HEAD_DIM, (h0 + 2) * SSM_HEAD_DIM)
                xs_p = xs_ref[rows, pcols]
                st_p = state_ref[:, pcols]
                rhs = jnp.concatenate([xs_p, st_p.astype(BF16)], axis=0)
                outs, sts, cds = [], [], []
                for h in (h0, h0 + 1):
                    colb = jnp.broadcast_to(acs[:, h:h + 1], (q, q))
                    rowb = acs_t[h:h + 1, :]
                    dtrow = dt_t[h:h + 1, :]
                    seg = jnp.exp(jnp.where(causal, colb - rowb, -jnp.inf))
                    m = cb * seg * dtrow
                    ce = cg_f * jnp.exp(colb)
                    lhs = jnp.concatenate([m, ce], axis=1).astype(BF16)
                    outs.append(_dot(lhs, rhs))
                    last = colb[q - 1:q, :]
                    wrow = jnp.exp(last - rowb) * dtrow
                    sts.append(_dot((bg_t * wrow).astype(BF16), xs_p))
                    cds.append(jnp.exp(last))
                y_p = (jnp.where(first_head, outs[0], outs[1])
                       + xs_p.astype(F32) * dskip_ref[:, pcols])
                y_ref[rows, pcols] = y_p
                state_ref[:, pcols] = (st_p * jnp.where(first_head[0:1, :], cds[0], cds[1])
                                       + jnp.where(first_head, sts[0], sts[1]))
        return carry_unused

    lax.fori_loop(0, tm // q, chunk, 0)

    zf = z_ref[...].astype(F32)
    gy = y_ref[...] * _silu(zf)
    r = lax.rsqrt(jnp.mean(gy * gy, axis=-1, keepdims=True) + NORM_EPS)
    yn = (gy * r * ng_ref[...]).astype(BF16)
    xo_ref[...] = x_ref[...] + g1_ref[...] * _dot(yn, wout_ref[...])


def _ssd(z, xbc, dt, x, g1, alog, dskip, ng, wout, tm):
    bsz, s, d = x.shape
    row = lambda n, cb=0: pl.BlockSpec((None, tm, n), lambda b, j: (b, j, cb))
    vec = lambda n: pl.BlockSpec((None, 1, n), lambda b, j: (b, 0, 0))
    return pl.pallas_call(
        functools.partial(_ssd_kernel, tm=tm),
        grid=(bsz, s // tm),
        in_specs=[row(SSM_D_INNER), row(SSM_D_INNER, 0), row(SSM_GN, 2), row(SSM_GN, 3),
                  row(LANES), row(d), vec(d), _const_spec(alog.shape),
                  _const_spec(dskip.shape), _const_spec(ng.shape), _const_spec(wout.shape)],
        out_specs=row(d),
        out_shape=jax.ShapeDtypeStruct((bsz, s, d), F32),
        scratch_shapes=[pltpu.VMEM((SSM_D_STATE, SSM_D_INNER), F32),
                        pltpu.VMEM((tm, SSM_D_INNER), F32)],
        compiler_params=_params(2),
        name="ssd",
    )(z, xbc, xbc, xbc, dt, x, g1, alog, dskip, ng, wout)


def _ffn_kernel(*refs, tm, attn_in, qkv_out, final_out):
    it = iter(refs)
    x_ref = next(it)
    if attn_in:
        o_ref, wo_ref, g1_ref = next(it), next(it), next(it)
    ng_ref, sh_ref, sc_ref, g2_ref = next(it), next(it), next(it), next(it)
    wg_ref, wv_ref, cw_ref, cb_ref, wout_ref = next(it), next(it), next(it), next(it), next(it)
    if qkv_out:
        kvg_ref, kvsh_ref, kvsc_ref, wk_ref, wvv_ref = (next(it) for _ in range(5))
        qg_ref, qsh_ref, qsc_ref, wq_ref = (next(it) for _ in range(4))
    if final_out:
        fg_ref, fsh_ref, fsc_ref = next(it), next(it), next(it)
    xo_ref = next(it)
    if qkv_out:
        q_ref, k_ref, v_ref = next(it), next(it), next(it)
    wbuf_g, wbuf_v, carry, acc = next(it), next(it), next(it), next(it)

    first = pl.program_id(1) == 0
    x = x_ref[...]
    if attn_in:
        x = x + g1_ref[...] * _dot(o_ref[...], wo_ref[...])
    h = _rms_mod(x, ng_ref[...], sh_ref[...], sc_ref[...]).astype(BF16)
    fc = FFN_COL_CHUNK
    for c in range(FFN_DIM // fc):
        gcols = slice(c * fc, (c + 1) * fc)
        vcols = slice(FFN_DIM + c * fc, FFN_DIM + (c + 1) * fc)
        ug = _causal_conv(_dot(h, wg_ref[:, gcols]), wbuf_g, carry, gcols,
                          cw_ref[:, gcols], cb_ref[:, gcols], first)
        uv = _causal_conv(_dot(h, wv_ref[:, gcols]), wbuf_v, carry, vcols,
                          cw_ref[:, vcols], cb_ref[:, vcols], first)
        part = _dot((_silu(ug) * uv).astype(BF16), wout_ref[gcols, :])
        if c == 0:
            acc[...] = part
        else:
            acc[...] += part
    xn = x + g2_ref[...] * acc[...]
    if final_out:
        xo_ref[...] = _rms_mod(xn, fg_ref[...], fsh_ref[...], fsc_ref[...])
    else:
        xo_ref[...] = xn
    if qkv_out:
        hk = _rms_mod(xn, kvg_ref[...], kvsh_ref[...], kvsc_ref[...]).astype(BF16)
        k_ref[...] = _dot(hk, wk_ref[...]).astype(BF16)
        v_ref[...] = _dot(hk, wvv_ref[...]).astype(BF16)
        hq = _rms_mod(xn, qg_ref[...], qsh_ref[...], qsc_ref[...]).astype(BF16)
        q_ref[...] = (_dot(hq, wq_ref[...]) * (1.0 / math.sqrt(SB_HEAD_DIM))).astype(BF16)


def _ffn(x, norm, ffn_w, tm, attn=None, qkv=None, final=None):
    bsz, s, d = x.shape
    row = lambda n: pl.BlockSpec((None, tm, n), lambda b, j: (b, j, 0))
    vec = lambda n: pl.BlockSpec((None, 1, n), lambda b, j: (b, 0, 0))
    g, sh, sc, g2 = norm
    args, specs = [x], [row(d)]
    if attn is not None:
        o, wo, g1 = attn
        args += [o, wo, g1]
        specs += [row(d), _const_spec(wo.shape), vec(d)]
    args += [g, sh, sc, g2]
    specs += [_const_spec(g.shape), vec(d), vec(d), vec(d)]
    args += list(ffn_w)
    specs += [_const_spec(w.shape) for w in ffn_w]
    out_shape = [jax.ShapeDtypeStruct((bsz, s, d), F32)]
    out_specs = [row(d)]
    if qkv is not None:
        kvg, kvsh, kvsc, wk, wv, qg, qsh, qsc, wq = qkv
        args += [kvg, kvsh, kvsc, wk, wv, qg, qsh, qsc, wq]
        specs += [_const_spec(kvg.shape), vec(d), vec(d), _const_spec(wk.shape),
                  _const_spec(wv.shape), _const_spec(qg.shape), vec(d), vec(d),
                  _const_spec(wq.shape)]
        out_shape += [jax.ShapeDtypeStruct((bsz, s, d), BF16)] * 3
        out_specs += [row(d)] * 3
    if final is not None:
        fg, fsh, fsc = final
        args += [fg, fsh, fsc]
        specs += [_const_spec(fg.shape), vec(d), vec(d)]
    return pl.pallas_call(
        functools.partial(_ffn_kernel, tm=tm, attn_in=attn is not None,
                          qkv_out=qkv is not None, final_out=final is not None),
        grid=(bsz, s // tm),
        in_specs=specs,
        out_specs=out_specs,
        out_shape=out_shape,
        scratch_shapes=[pltpu.VMEM((tm + SUBLANES, FFN_COL_CHUNK), F32),
                        pltpu.VMEM((tm + SUBLANES, FFN_COL_CHUNK), F32),
                        pltpu.VMEM((SUBLANES, 2 * FFN_DIM), F32),
                        pltpu.VMEM((tm, d), F32)],
        compiler_params=_params(2),
        name="ffn_qkv" if qkv is not None else "ffn_final",
    )(*args)


def _sb_kernel(q_ref, k_ref, v_ref, u2_ref, o_ref, *, seq):
    blk = SB_BLOCK
    ri = lax.broadcasted_iota(jnp.int32, (blk, blk), 0)
    ci = lax.broadcasted_iota(jnp.int32, (blk, blk), 1)
    live = ci < ri
    first_head = ci < SB_HEAD_DIM
    u2 = u2_ref[...]

    def head_step(qh, k, v, suffix, masked):
        z = _dot_nt(qh, k)
        sp = _softplus(z)
        log_1mb = -sp
        if masked:
            log_1mb = jnp.where(live, log_1mb, 0.0)
        r = _dot(jnp.concatenate(_split2(log_1mb), axis=1), u2)
        later, row_total = r[:, 0:blk], r[:, blk:2 * blk]
        w = jnp.exp((z - sp) + later + suffix)
        if masked:
            w = jnp.where(live, w, 0.0)
        return _dot(w.astype(BF16), v), suffix + row_total

    def pair_step(kb, qa, qb, carry, masked):
        acc_a, acc_b, suf_a, suf_b = carry
        rows = pl.ds(pl.multiple_of(kb * blk, blk), blk)
        k = k_ref[rows, :]
        v = v_ref[rows, :]
        pa, suf_a = head_step(qa, k, v, suf_a, masked)
        pb, suf_b = head_step(qb, k, v, suf_b, masked)
        return acc_a + pa, acc_b + pb, suf_a, suf_b

    def q_block(qi, carry_unused):
        rows = pl.ds(pl.multiple_of(qi * blk, blk), blk)
        q = q_ref[rows, :]
        zero_q = jnp.zeros_like(q)
        qa = jnp.where(first_head, q, zero_q)
        qb = jnp.where(first_head, zero_q, q)
        zeros = jnp.zeros((blk, blk), F32)
        carry = pair_step(qi, qa, qb, (zeros, zeros, zeros, zeros), True)

        def cond(state):
            kb, (_, _, suf_a, suf_b) = state
            alive = jnp.max(jnp.maximum(suf_a, suf_b)) > SB_LOG_ZERO
            return jnp.logical_and(kb >= 0, alive)

        def body(state):
            kb, carry = state
            return kb - 1, pair_step(kb, qa, qb, carry, False)

        _, (acc_a, acc_b, _, _) = lax.while_loop(cond, body, (qi - 1, carry))
        o_ref[rows, :] = jnp.where(first_head, acc_a, acc_b).astype(BF16)
        return carry_unused

    lax.fori_loop(0, seq // blk, q_block, 0)


def _sb_attn(q, k, v, u2):
    bsz, s, d = q.shape
    col = pl.BlockSpec((None, s, LANES), lambda b, p: (b, 0, p))
    return pl.pallas_call(
        functools.partial(_sb_kernel, seq=s),
        grid=(bsz, d // LANES),
        in_specs=[col, col, col, _const_spec(u2.shape)],
        out_specs=col,
        out_shape=jax.ShapeDtypeStruct((bsz, s, d), BF16),
        compiler_params=_params(2),
        name="sb_attn",
    )(q, k, v, u2)


def kernel(x, c, ada_w, ada_b, norm_mix_g, norm_ffn_g, ssm_w_in, ssm_conv_w, ssm_conv_b, ssm_dt_bias, ssm_a_log, ssm_d, ssm_norm_g, ssm_w_out, kv_ada_w, kv_ada_b, kv_norm_g, w_k, w_v, sb_w_q, sb_w_o, ffn_w_in, ffn_conv_w, ffn_conv_b, ffn_w_out, final_ada_w, final_ada_b, final_norm_g):
    bsz, s, d = x.shape
    tm = min(TOKEN_TILE, s)
    assert d == D_MODEL and s % tm == 0 and tm % SSM_CHUNK == 0 and bsz <= SUBLANES

    c_pad = jnp.zeros((SUBLANES, d), F32).at[:bsz].set(c)
    mod = _mods(c_pad, ada_w, ada_b[:, None, :])
    kv_mod = _mods(c_pad, kv_ada_w[None], kv_ada_b[None, None, :])[0]
    f_mod = _mods(c_pad, final_ada_w[None], final_ada_b[None, None, :])[0]
    part = lambda m, i: m[:bsz, None, i * d:(i + 1) * d]
    sh1a, sc1a, g1a, sh2a, sc2a, g2a = (part(mod[0], i) for i in range(6))
    sh1b, sc1b, g1b, sh2b, sc2b, g2b = (part(mod[1], i) for i in range(6))
    kv_sh, kv_sc = part(kv_mod, 0), part(kv_mod, 1)
    f_sh, f_sc = part(f_mod, 0), part(f_mod, 1)
    row = lambda a: a.reshape(1, -1)

    w_in = ssm_w_in[0]
    wz = w_in[:, :SSM_D_INNER].astype(BF16)
    wx = w_in[:, SSM_D_INNER:SSM_D_INNER + SSM_CONV_DIM].astype(BF16)
    pad_h = LANES - SSM_N_HEADS
    wdt = jnp.pad(w_in[:, SSM_D_INNER + SSM_CONV_DIM:], ((0, 0), (0, pad_h)))
    wdt_hi, wdt_lo = _split2(wdt)
    dtb = jnp.pad(ssm_dt_bias[0], (0, pad_h)).reshape(1, LANES)
    z, xbc, dt = _ssm_in(x, row(norm_mix_g[0]), sh1a, sc1a, wz, wx, wdt_hi, wdt_lo,
                         ssm_conv_w[0], row(ssm_conv_b[0]), dtb, tm)
    alog = jnp.pad(ssm_a_log[0], (0, pad_h)).reshape(1, LANES)
    dskip = jnp.repeat(ssm_d[0], SSM_HEAD_DIM).reshape(1, SSM_D_INNER)
    x = _ssd(z, xbc, dt, x, g1a, alog, dskip, row(ssm_norm_g[0]),
             ssm_w_out[0].astype(BF16), tm)

    def ffn_weights(layer):
        w = ffn_w_in[layer]
        return (w[:, :FFN_DIM].astype(BF16), w[:, FFN_DIM:].astype(BF16), ffn_conv_w[layer],
                row(ffn_conv_b[layer]), ffn_w_out[layer].astype(BF16))

    x, q, k, v = _ffn(x, (row(norm_ffn_g[0]), sh2a, sc2a, g2a), ffn_weights(0), tm,
                      qkv=(row(kv_norm_g), kv_sh, kv_sc, w_k.astype(BF16), w_v.astype(BF16),
                           row(norm_mix_g[1]), sh1b, sc1b, sb_w_q[0].astype(BF16)))

    ji = lax.broadcasted_iota(jnp.int32, (SB_BLOCK, 2 * SB_BLOCK), 0)
    si = lax.broadcasted_iota(jnp.int32, (SB_BLOCK, 2 * SB_BLOCK), 1)
    u1 = jnp.where(jnp.logical_or(si >= SB_BLOCK, ji > si), 1.0, 0.0).astype(BF16)
    u2 = jnp.concatenate([u1, u1], axis=0)
    o = _sb_attn(q, k, v, u2)
    return _ffn(x, (row(norm_ffn_g[1]), sh2b, sc2b, g2b), ffn_weights(1), tm,
                attn=(o, sb_w_o[0].astype(BF16), g1b),
                final=(row(final_norm_g), f_sh, f_sc))[0]
```

```python
import functools
import math

import jax
import jax.numpy as jnp
from jax import lax
from jax.experimental import pallas as pl
from jax.experimental.pallas import tpu as pltpu

F32 = jnp.float32
BF16 = jnp.bfloat16

D_MODEL = 1024
SSM_D_INNER = 2 * D_MODEL
SSM_HEAD_DIM = 64
SSM_N_HEADS = SSM_D_INNER // SSM_HEAD_DIM
SSM_D_STATE = 128
SSM_N_GROUPS = 8
SSM_HEADS_PER_GROUP = SSM_N_HEADS // SSM_N_GROUPS
SSM_CONV = 4
SSM_CHUNK = 128
SSM_GN = SSM_N_GROUPS * SSM_D_STATE
SSM_CONV_DIM = SSM_D_INNER + 2 * SSM_GN
SB_N_HEADS = 16
SB_HEAD_DIM = D_MODEL // SB_N_HEADS
SB_BLOCK = 128
FFN_DIM = ((8 * D_MODEL // 3 + 255) // 256) * 256
FFN_CONV = 3
NORM_EPS = 1e-6

LANES = 128
SUBLANES = 8
VMEM_LIMIT = 56 * 1024 * 1024
TOKEN_TILE = 512
FFN_COL_CHUNK = 256
SSM_COL_CHUNK = 512
SB_LOG2_ZERO = -110.0 / math.log(2.0)
SB_Q_SCALE = math.log2(math.e) / math.sqrt(SB_HEAD_DIM)
SB_Q_TILE = 2 * SB_BLOCK
SB_PAIRS_PER_STEP = 2


def _dot(a, b):
    return jnp.dot(a, b, preferred_element_type=F32)


def _dot_nt(a, b):
    return lax.dot_general(a, b, (((1,), (1,)), ((), ())), preferred_element_type=F32)


def _split2(x):
    hi = x.astype(BF16)
    lo = (x - hi.astype(F32)).astype(BF16)
    return hi, lo


def _split3(x):
    hi = x.astype(BF16)
    r = x - hi.astype(F32)
    mid = r.astype(BF16)
    lo = (r - mid.astype(F32)).astype(BF16)
    return hi, mid, lo


def _sigmoid(x):
    return 1.0 / (1.0 + jnp.exp(-x))


def _silu(x):
    return x * _sigmoid(x)


def _softplus(x):
    return jnp.maximum(x, 0.0) + jnp.log(1.0 + jnp.exp(-jnp.abs(x)))


def _rms_mod(x, g, sh, sc):
    r = lax.rsqrt(jnp.mean(x * x, axis=-1, keepdims=True) + NORM_EPS)
    return (x * r * g) * (1.0 + sc) + sh


def _const_spec(shape):
    n = len(shape)
    return pl.BlockSpec(shape, lambda *_: (0,) * n, pipeline_mode=pl.Buffered(1))


def _params(n_axes):
    return pltpu.CompilerParams(dimension_semantics=("arbitrary",) * n_axes,
                                vmem_limit_bytes=VMEM_LIMIT)


def _mods_kernel(c_ref, w_ref, b_ref, o_ref):
    ca = _silu(c_ref[...])
    c_hi, c_lo = _split2(ca)
    w_hi, w_lo = _split2(w_ref[...])
    o_ref[...] = _dot(c_hi, w_hi) + _dot(c_hi, w_lo) + _dot(c_lo, w_hi) + b_ref[...]


def _mods(c_pad, w, b, tn=1024):
    nl, d, n = w.shape
    return pl.pallas_call(
        _mods_kernel,
        grid=(nl, n // tn),
        in_specs=[pl.BlockSpec((SUBLANES, d), lambda l, j: (0, 0)),
                  pl.BlockSpec((None, d, tn), lambda l, j: (l, 0, j)),
                  pl.BlockSpec((None, 1, tn), lambda l, j: (l, 0, j))],
        out_specs=pl.BlockSpec((None, SUBLANES, tn), lambda l, j: (l, 0, j)),
        out_shape=jax.ShapeDtypeStruct((nl, SUBLANES, n), F32),
        compiler_params=_params(2),
        name="mods",
    )(c_pad, w, b)


def _causal_conv(u, wbuf, carry, cols, cw, cb, first):
    tm = u.shape[0]
    width = cw.shape[0]

    @pl.when(first)
    def _():
        carry[:, cols] = jnp.zeros((SUBLANES, u.shape[1]), F32)

    wbuf[0:SUBLANES, :] = carry[:, cols]
    wbuf[SUBLANES:SUBLANES + tm, :] = u
    carry[:, cols] = wbuf[tm:tm + SUBLANES, :]
    acc = cb + cw[width - 1:width, :] * u
    for k in range(width - 1):
        off = SUBLANES - (width - 1) + k
        acc = acc + cw[k:k + 1, :] * wbuf[off:off + tm, :]
    return acc


def _ssm_in_kernel(x_ref, g_ref, sh_ref, sc_ref, wz_ref, wx_ref, wdt_hi_ref, wdt_lo_ref,
                   cw_ref, cb_ref, dtb_ref, z_ref, xbc_ref, dt_ref, wbuf, carry):
    first = pl.program_id(1) == 0
    hf = _rms_mod(x_ref[...], g_ref[...], sh_ref[...], sc_ref[...])
    h_hi, h_lo = _split2(hf)
    z_ref[...] = _dot(h_hi, wz_ref[...]).astype(BF16)
    dt_raw = (_dot(h_hi, wdt_hi_ref[...]) + _dot(h_hi, wdt_lo_ref[...])
              + _dot(h_lo, wdt_hi_ref[...]))
    dt_ref[...] = _softplus(dt_raw + dtb_ref[...])
    for c in range(SSM_CONV_DIM // SSM_COL_CHUNK):
        cols = slice(c * SSM_COL_CHUNK, (c + 1) * SSM_COL_CHUNK)
        u = _dot(h_hi, wx_ref[:, cols])
        v = _causal_conv(u, wbuf, carry, cols, cw_ref[:, cols], cb_ref[:, cols], first)
        xbc_ref[:, cols] = _silu(v).astype(BF16)


def _ssm_in(x, g, sh, sc, wz, wx, wdt_hi, wdt_lo, cw, cb, dtb, tm):
    bsz, s, d = x.shape
    row = lambda n: pl.BlockSpec((None, tm, n), lambda b, j: (b, j, 0))
    vec = lambda n: pl.BlockSpec((None, 1, n), lambda b, j: (b, 0, 0))
    return pl.pallas_call(
        _ssm_in_kernel,
        grid=(bsz, s // tm),
        in_specs=[row(d), _const_spec((1, d)), vec(d), vec(d),
                  _const_spec(wz.shape), _const_spec(wx.shape),
                  _const_spec(wdt_hi.shape), _const_spec(wdt_lo.shape),
                  _const_spec(cw.shape), _const_spec(cb.shape), _const_spec(dtb.shape)],
        out_specs=[row(SSM_D_INNER), row(SSM_CONV_DIM), row(LANES)],
        out_shape=[jax.ShapeDtypeStruct((bsz, s, SSM_D_INNER), BF16),
                   jax.ShapeDtypeStruct((bsz, s, SSM_CONV_DIM), BF16),
                   jax.ShapeDtypeStruct((bsz, s, LANES), F32)],
        scratch_shapes=[pltpu.VMEM((tm + SUBLANES, SSM_COL_CHUNK), F32),
                        pltpu.VMEM((SUBLANES, SSM_CONV_DIM), F32)],
        compiler_params=_params(2),
        name="ssm_in",
    )(x, g, sh, sc, wz, wx, wdt_hi, wdt_lo, cw, cb, dtb)


def _ssd_kernel(z_ref, xs_ref, b_ref, c_ref, dt_ref, x_ref, g1_ref, alog_ref, dskip_ref,
                ng_ref, wout_ref, xo_ref, state_ref, y_ref, *, tm):
    q = SSM_CHUNK

    @pl.when(pl.program_id(1) == 0)
    def _():
        state_ref[...] = jnp.zeros(state_ref.shape, F32)

    a_neg = -jnp.exp(alog_ref[...])
    ri = lax.broadcasted_iota(jnp.int32, (q, q), 0)
    ci = lax.broadcasted_iota(jnp.int32, (q, q), 1)
    causal = ri >= ci
    tril = jnp.where(causal, 1.0, 0.0).astype(BF16)
    first_head = ci < SSM_HEAD_DIM

    def chunk(ck, carry_unused):
        rows = pl.ds(pl.multiple_of(ck * q, q), q)
        dtc = dt_ref[rows, :]
        a3 = jnp.concatenate(_split3(dtc * a_neg), axis=1)
        r = _dot(tril, a3)
        acs = r[:, 0:LANES] + r[:, LANES:2 * LANES] + r[:, 2 * LANES:3 * LANES]
        acs_t = acs.T
        dt_t = dtc.T
        for grp in range(SSM_N_GROUPS):
            gcols = slice(grp * SSM_D_STATE, (grp + 1) * SSM_D_STATE)
            bg = b_ref[rows, gcols]
            cg = c_ref[rows, gcols]
            cb = _dot_nt(cg, bg)
            bg_t = bg.astype(F32).T
            cg_f = cg.astype(F32)
            for pr in range(SSM_HEADS_PER_GROUP // 2):
                h0 = grp * SSM_HEADS_PER_GROUP + 2 * pr
                pcols = slice(h0 * SSM_HEAD_DIM, (h0 + 2) * SSM_HEAD_DIM)
                xs_p = xs_ref[rows, pcols]
                st_p = state_ref[:, pcols]
                rhs = jnp.concatenate([xs_p, st_p.astype(BF16)], axis=0)
                outs, sts, cds = [], [], []
                for h in (h0, h0 + 1):
                    colb = jnp.broadcast_to(acs[:, h:h + 1], (q, q))
                    rowb = acs_t[h:h + 1, :]
                    dtrow = dt_t[h:h + 1, :]
                    seg = jnp.exp(jnp.where(causal, colb - rowb, -jnp.inf))
                    m = cb * seg * dtrow
                    ce = cg_f * jnp.exp(colb)
                    lhs = jnp.concatenate([m, ce], axis=1).astype(BF16)
                    outs.append(_dot(lhs, rhs))
                    last = colb[q - 1:q, :]
                    wrow = jnp.exp(last - rowb) * dtrow
                    sts.append(_dot((bg_t * wrow).astype(BF16), xs_p))
                    cds.append(jnp.exp(last))
                y_p = (jnp.where(first_head, outs[0], outs[1])
                       + xs_p.astype(F32) * dskip_ref[:, pcols])
                y_ref[rows, pcols] = y_p
                state_ref[:, pcols] = (st_p * jnp.where(first_head[0:1, :], cds[0], cds[1])
                                       + jnp.where(first_head, sts[0], sts[1]))
        return carry_unused

    lax.fori_loop(0, tm // q, chunk, 0)

    zf = z_ref[...].astype(F32)
    gy = y_ref[...] * _silu(zf)
    r = lax.rsqrt(jnp.mean(gy * gy, axis=-1, keepdims=True) + NORM_EPS)
    yn = (gy * r * ng_ref[...]).astype(BF16)
    xo_ref[...] = x_ref[...] + g1_ref[...] * _dot(yn, wout_ref[...])


def _ssd(z, xbc, dt, x, g1, alog, dskip, ng, wout, tm):
    bsz, s, d = x.shape
    row = lambda n, cb=0: pl.BlockSpec((None, tm, n), lambda b, j: (b, j, cb))
    vec = lambda n: pl.BlockSpec((None, 1, n), lambda b, j: (b, 0, 0))
    return pl.pallas_call(
        functools.partial(_ssd_kernel, tm=tm),
        grid=(bsz, s // tm),
        in_specs=[row(SSM_D_INNER), row(SSM_D_INNER, 0), row(SSM_GN, 2), row(SSM_GN, 3),
                  row(LANES), row(d), vec(d), _const_spec(alog.shape),
                  _const_spec(dskip.shape), _const_spec(ng.shape), _const_spec(wout.shape)],
        out_specs=row(d),
        out_shape=jax.ShapeDtypeStruct((bsz, s, d), F32),
        scratch_shapes=[pltpu.VMEM((SSM_D_STATE, SSM_D_INNER), F32),
                        pltpu.VMEM((tm, SSM_D_INNER), F32)],
        compiler_params=_params(2),
        name="ssd",
    )(z, xbc, xbc, xbc, dt, x, g1, alog, dskip, ng, wout)


def _ffn_kernel(*refs, tm, attn_in, qkv_out, final_out):
    it = iter(refs)
    x_ref = next(it)
    if attn_in:
        o_ref, wo_ref, g1_ref = next(it), next(it), next(it)
    ng_ref, sh_ref, sc_ref, g2_ref = next(it), next(it), next(it), next(it)
    wg_ref, wv_ref, cw_ref, cb_ref, wout_ref = next(it), next(it), next(it), next(it), next(it)
    if qkv_out:
        kvg_ref, kvsh_ref, kvsc_ref, wk_ref, wvv_ref = (next(it) for _ in range(5))
        qg_ref, qsh_ref, qsc_ref, wq_ref = (next(it) for _ in range(4))
    if final_out:
        fg_ref, fsh_ref, fsc_ref = next(it), next(it), next(it)
    xo_ref = next(it)
    if qkv_out:
        q_ref, k_ref, v_ref = next(it), next(it), next(it)
    wbuf_g, wbuf_v, carry, acc = next(it), next(it), next(it), next(it)

    first = pl.program_id(1) == 0
    x = x_ref[...]
    if attn_in:
        x = x + g1_ref[...] * _dot(o_ref[...], wo_ref[...])
    h = _rms_mod(x, ng_ref[...], sh_ref[...], sc_ref[...]).astype(BF16)
    fc = FFN_COL_CHUNK
    for c in range(FFN_DIM // fc):
        gcols = slice(c * fc, (c + 1) * fc)
        vcols = slice(FFN_DIM + c * fc, FFN_DIM + (c + 1) * fc)
        ug = _causal_conv(_dot(h, wg_ref[:, gcols]), wbuf_g, carry, gcols,
                          cw_ref[:, gcols], cb_ref[:, gcols], first)
        uv = _causal_conv(_dot(h, wv_ref[:, gcols]), wbuf_v, carry, vcols,
                          cw_ref[:, vcols], cb_ref[:, vcols], first)
        part = _dot((_silu(ug) * uv).astype(BF16), wout_ref[gcols, :])
        if c == 0:
            acc[...] = part
        else:
            acc[...] += part
    xn = x + g2_ref[...] * acc[...]
    if final_out:
        xo_ref[...] = _rms_mod(xn, fg_ref[...], fsh_ref[...], fsc_ref[...])
    else:
        xo_ref[...] = xn
    if qkv_out:
        hk = _rms_mod(xn, kvg_ref[...], kvsh_ref[...], kvsc_ref[...]).astype(BF16)
        k_ref[...] = _dot(hk, wk_ref[...]).astype(BF16)
        v_ref[...] = _dot(hk, wvv_ref[...]).astype(BF16)
        hq = _rms_mod(xn, qg_ref[...], qsh_ref[...], qsc_ref[...]).astype(BF16)
        q_ref[...] = (_dot(hq, wq_ref[...]) * SB_Q_SCALE).astype(BF16)


def _ffn(x, norm, ffn_w, tm, attn=None, qkv=None, final=None):
    bsz, s, d = x.shape
    row = lambda n: pl.BlockSpec((None, tm, n), lambda b, j: (b, j, 0))
    vec = lambda n: pl.BlockSpec((None, 1, n), lambda b, j: (b, 0, 0))
    g, sh, sc, g2 = norm
    args, specs = [x], [row(d)]
    if attn is not None:
        o, wo, g1 = attn
        args += [o, wo, g1]
        specs += [row(d), _const_spec(wo.shape), vec(d)]
    args += [g, sh, sc, g2]
    specs += [_const_spec(g.shape), vec(d), vec(d), vec(d)]
    args += list(ffn_w)
    specs += [_const_spec(w.shape) for w in ffn_w]
    out_shape = [jax.ShapeDtypeStruct((bsz, s, d), F32)]
    out_specs = [row(d)]
    if qkv is not None:
        kvg, kvsh, kvsc, wk, wv, qg, qsh, qsc, wq = qkv
        args += [kvg, kvsh, kvsc, wk, wv, qg, qsh, qsc, wq]
        specs += [_const_spec(kvg.shape), vec(d), vec(d), _const_spec(wk.shape),
                  _const_spec(wv.shape), _const_spec(qg.shape), vec(d), vec(d),
                  _const_spec(wq.shape)]
        out_shape += [jax.ShapeDtypeStruct((bsz, s, d), BF16)] * 3
        out_specs += [row(d)] * 3
    if final is not None:
        fg, fsh, fsc = final
        args += [fg, fsh, fsc]
        specs += [_const_spec(fg.shape), vec(d), vec(d)]
    return pl.pallas_call(
        functools.partial(_ffn_kernel, tm=tm, attn_in=attn is not None,
                          qkv_out=qkv is not None, final_out=final is not None),
        grid=(bsz, s // tm),
        in_specs=specs,
        out_specs=out_specs,
        out_shape=out_shape,
        scratch_shapes=[pltpu.VMEM((tm + SUBLANES, FFN_COL_CHUNK), F32),
                        pltpu.VMEM((tm + SUBLANES, FFN_COL_CHUNK), F32),
                        pltpu.VMEM((SUBLANES, 2 * FFN_DIM), F32),
                        pltpu.VMEM((tm, d), F32)],
        compiler_params=_params(2),
        name="ffn_qkv" if qkv is not None else "ffn_final",
    )(*args)


def _sb_kernel(q_ref, k_ref, v_ref, u2_ref, o_ref, *, seq):
    blk = SB_BLOCK
    tq = SB_Q_TILE
    key = lax.broadcasted_iota(jnp.int32, (tq, 2 * blk), 1) & (blk - 1)
    qrow = lax.broadcasted_iota(jnp.int32, (tq, 2 * blk), 0)
    live = key < qrow
    first_head = lax.broadcasted_iota(jnp.int32, (blk, blk), 1) < SB_HEAD_DIM
    u2 = u2_ref[...]

    def block_diag(t):
        zero = jnp.zeros_like(t)
        return jnp.concatenate([jnp.where(first_head, t, zero), jnp.where(first_head, zero, t)],
                               axis=0)

    def step(q, cols, kb, suffix, mask):
        rows = pl.ds(pl.multiple_of(kb * blk, blk), blk)
        z = _dot_nt(q, block_diag(k_ref[rows, cols]))
        log_b = jnp.minimum(z, 0.0) - jnp.log2(1.0 + jnp.exp2(-jnp.abs(z)))
        log_1mb = log_b - z
        if mask is not None:
            log_1mb = jnp.where(mask, log_1mb, 0.0)
        hi, lo = _split2(log_1mb)
        ra = _dot(jnp.concatenate([hi[:, :blk], lo[:, :blk]], axis=1), u2)
        rb = _dot(jnp.concatenate([hi[:, blk:], lo[:, blk:]], axis=1), u2)
        later = jnp.concatenate([ra[:, :blk], rb[:, :blk]], axis=1)
        row_total = jnp.concatenate([ra[:, blk:], rb[:, blk:]], axis=1)
        w = jnp.exp2(log_b + later + suffix)
        if mask is not None:
            w = jnp.where(mask, w, 0.0)
        return _dot(w.astype(BF16), block_diag(v_ref[rows, cols])), suffix + row_total

    def diag_tile(qi, q, cols):
        zeros = jnp.zeros((blk, 2 * blk), F32)
        pv_up, suf_up = step(q[blk:, :], cols, 2 * qi + 1, zeros, live[:blk, :])
        pv, suffix = step(q, cols, 2 * qi, jnp.concatenate([zeros, suf_up], axis=0), live)
        return pv + jnp.concatenate([jnp.zeros((blk, blk), F32), pv_up], axis=0), suffix

    def full_tile(kt, q, cols, acc, suffix):
        pv_a, suffix = step(q, cols, 2 * kt + 1, suffix, None)
        pv_b, suffix = step(q, cols, 2 * kt, suffix, None)
        return acc + pv_a + pv_b, suffix

    def alive(suffixes):
        top = functools.reduce(jnp.maximum, suffixes)
        return (jnp.max(top) > SB_LOG2_ZERO).astype(jnp.int32)

    pairs = [slice(p * LANES, (p + 1) * LANES) for p in range(q_ref.shape[-1] // LANES)]

    for cols in pairs:
        acc, _ = diag_tile(0, q_ref[0:tq, cols], cols)
        o_ref[0:tq, cols] = acc.astype(BF16)

    def q_tile(qi, carry_unused):
        rows = pl.ds(pl.multiple_of(qi * tq, tq), tq)
        qs = [q_ref[rows, cols] for cols in pairs]
        accs, sufs = [], []
        for q, cols in zip(qs, pairs):
            acc, suffix = diag_tile(qi, q, cols)
            acc, suffix = full_tile(qi - 1, q, cols, acc, suffix)
            accs.append(acc)
            sufs.append(suffix)

        def cond(state):
            return jnp.logical_and(state[0] >= 0, state[3] > 0)

        def body(state):
            kt, accs, sufs, _ = state
            out = [full_tile(kt, q, cols, acc, suffix)
                   for q, cols, acc, suffix in zip(qs, pairs, accs, sufs)]
            sufs = [o[1] for o in out]
            return kt - 1, [o[0] for o in out], sufs, alive(sufs)

        _, accs, _, _ = lax.while_loop(cond, body, (qi - 2, accs, sufs, alive(sufs)))
        for cols, acc in zip(pairs, accs):
            o_ref[rows, cols] = acc.astype(BF16)
        return carry_unused

    lax.fori_loop(1, seq // tq, q_tile, 0)


def _sb_attn(q, k, v, u2):
    bsz, s, d = q.shape
    width = SB_PAIRS_PER_STEP * LANES
    col = pl.BlockSpec((None, s, width), lambda b, p: (b, 0, p))
    return pl.pallas_call(
        functools.partial(_sb_kernel, seq=s),
        grid=(bsz, d // width),
        in_specs=[col, col, col, _const_spec(u2.shape)],
        out_specs=col,
        out_shape=jax.ShapeDtypeStruct((bsz, s, d), BF16),
        compiler_params=_params(2),
        name="sb_attn",
    )(q, k, v, u2)


def kernel(x, c, ada_w, ada_b, norm_mix_g, norm_ffn_g, ssm_w_in, ssm_conv_w, ssm_conv_b, ssm_dt_bias, ssm_a_log, ssm_d, ssm_norm_g, ssm_w_out, kv_ada_w, kv_ada_b, kv_norm_g, w_k, w_v, sb_w_q, sb_w_o, ffn_w_in, ffn_conv_w, ffn_conv_b, ffn_w_out, final_ada_w, final_ada_b, final_norm_g):
    bsz, s, d = x.shape
    tm = min(TOKEN_TILE, s)
    assert d == D_MODEL and s % tm == 0 and tm % SSM_CHUNK == 0 and bsz <= SUBLANES

    c_pad = jnp.zeros((SUBLANES, d), F32).at[:bsz].set(c)
    mod = _mods(c_pad, ada_w, ada_b[:, None, :])
    kv_mod = _mods(c_pad, kv_ada_w[None], kv_ada_b[None, None, :])[0]
    f_mod = _mods(c_pad, final_ada_w[None], final_ada_b[None, None, :])[0]
    part = lambda m, i: m[:bsz, None, i * d:(i + 1) * d]
    sh1a, sc1a, g1a, sh2a, sc2a, g2a = (part(mod[0], i) for i in range(6))
    sh1b, sc1b, g1b, sh2b, sc2b, g2b = (part(mod[1], i) for i in range(6))
    kv_sh, kv_sc = part(kv_mod, 0), part(kv_mod, 1)
    f_sh, f_sc = part(f_mod, 0), part(f_mod, 1)
    row = lambda a: a.reshape(1, -1)

    w_in = ssm_w_in[0]
    wz = w_in[:, :SSM_D_INNER].astype(BF16)
    wx = w_in[:, SSM_D_INNER:SSM_D_INNER + SSM_CONV_DIM].astype(BF16)
    pad_h = LANES - SSM_N_HEADS
    wdt = jnp.pad(w_in[:, SSM_D_INNER + SSM_CONV_DIM:], ((0, 0), (0, pad_h)))
    wdt_hi, wdt_lo = _split2(wdt)
    dtb = jnp.pad(ssm_dt_bias[0], (0, pad_h)).reshape(1, LANES)
    z, xbc, dt = _ssm_in(x, row(norm_mix_g[0]), sh1a, sc1a, wz, wx, wdt_hi, wdt_lo,
                         ssm_conv_w[0], row(ssm_conv_b[0]), dtb, tm)
    alog = jnp.pad(ssm_a_log[0], (0, pad_h)).reshape(1, LANES)
    dskip = jnp.repeat(ssm_d[0], SSM_HEAD_DIM).reshape(1, SSM_D_INNER)
    x = _ssd(z, xbc, dt, x, g1a, alog, dskip, row(ssm_norm_g[0]),
             ssm_w_out[0].astype(BF16), tm)

    def ffn_weights(layer):
        w = ffn_w_in[layer]
        return (w[:, :FFN_DIM].astype(BF16), w[:, FFN_DIM:].astype(BF16), ffn_conv_w[layer],
                row(ffn_conv_b[layer]), ffn_w_out[layer].astype(BF16))

    x, q, k, v = _ffn(x, (row(norm_ffn_g[0]), sh2a, sc2a, g2a), ffn_weights(0), tm,
                      qkv=(row(kv_norm_g), kv_sh, kv_sc, w_k.astype(BF16), w_v.astype(BF16),
                           row(norm_mix_g[1]), sh1b, sc1b, sb_w_q[0].astype(BF16)))

    ji = lax.broadcasted_iota(jnp.int32, (SB_BLOCK, 2 * SB_BLOCK), 0)
    si = lax.broadcasted_iota(jnp.int32, (SB_BLOCK, 2 * SB_BLOCK), 1)
    u1 = jnp.where(jnp.logical_or(si >= SB_BLOCK, ji > si), 1.0, 0.0).astype(BF16)
    u2 = jnp.concatenate([u1, u1], axis=0)
    o = _sb_attn(q, k, v, u2)
    return _ffn(x, (row(norm_ffn_g[1]), sh2b, sc2b, g2b), ffn_weights(1), tm,
                attn=(o, sb_w_o[0].astype(BF16), g1b),
                final=(row(final_norm_g), f_sh, f_sc))[0]
```

```python
import functools
import math

import jax
import jax.numpy as jnp
from jax import lax
from jax.experimental import pallas as pl
from jax.experimental.pallas import tpu as pltpu

F32 = jnp.float32
BF16 = jnp.bfloat16

D_MODEL = 1024
SSM_D_INNER = 2 * D_MODEL
SSM_HEAD_DIM = 64
SSM_N_HEADS = SSM_D_INNER // SSM_HEAD_DIM
SSM_D_STATE = 128
SSM_N_GROUPS = 8
SSM_HEADS_PER_GROUP = SSM_N_HEADS // SSM_N_GROUPS
SSM_CONV = 4
SSM_CHUNK = 128
SSM_GN = SSM_N_GROUPS * SSM_D_STATE
SSM_CONV_DIM = SSM_D_INNER + 2 * SSM_GN
SB_N_HEADS = 16
SB_HEAD_DIM = D_MODEL // SB_N_HEADS
SB_BLOCK = 128
FFN_DIM = ((8 * D_MODEL // 3 + 255) // 256) * 256
FFN_CONV = 3
NORM_EPS = 1e-6

LANES = 128
SUBLANES = 8
VMEM_LIMIT = 56 * 1024 * 1024
TOKEN_TILE = 512
FFN_COL_CHUNK = 256
SSM_COL_CHUNK = 256
SB_LOG2_ZERO = -110.0 / math.log(2.0)
SB_Q_SCALE = math.log2(math.e) / math.sqrt(SB_HEAD_DIM)
SB_Q_TILE = 2 * SB_BLOCK
SB_PAIRS_PER_STEP = 2


def _dot(a, b):
    return jnp.dot(a, b, preferred_element_type=F32)


def _dot_nt(a, b):
    return lax.dot_general(a, b, (((1,), (1,)), ((), ())), preferred_element_type=F32)


def _split2(x):
    hi = x.astype(BF16)
    lo = (x - hi.astype(F32)).astype(BF16)
    return hi, lo


def _split3(x):
    hi = x.astype(BF16)
    r = x - hi.astype(F32)
    mid = r.astype(BF16)
    lo = (r - mid.astype(F32)).astype(BF16)
    return hi, mid, lo


def _sigmoid(x):
    return 1.0 / (1.0 + jnp.exp(-x))


def _silu(x):
    return x * _sigmoid(x)


def _softplus(x):
    return jnp.maximum(x, 0.0) + jnp.log(1.0 + jnp.exp(-jnp.abs(x)))


def _rms_mod(x, g, sh, sc):
    r = lax.rsqrt(jnp.mean(x * x, axis=-1, keepdims=True) + NORM_EPS)
    return (x * r * g) * (1.0 + sc) + sh


def _const_spec(shape):
    n = len(shape)
    return pl.BlockSpec(shape, lambda *_: (0,) * n, pipeline_mode=pl.Buffered(1))


def _params(n_axes):
    return pltpu.CompilerParams(dimension_semantics=("arbitrary",) * n_axes,
                                vmem_limit_bytes=VMEM_LIMIT)


def _mods_kernel(c_ref, w_ref, b_ref, o_ref):
    ca = _silu(c_ref[...])
    c_hi, c_lo = _split2(ca)
    w_hi, w_lo = _split2(w_ref[...])
    o_ref[...] = _dot(c_hi, w_hi) + _dot(c_hi, w_lo) + _dot(c_lo, w_hi) + b_ref[...]


def _mods(c_pad, w, b, tn=1024):
    nl, d, n = w.shape
    return pl.pallas_call(
        _mods_kernel,
        grid=(nl, n // tn),
        in_specs=[pl.BlockSpec((SUBLANES, d), lambda l, j: (0, 0)),
                  pl.BlockSpec((None, d, tn), lambda l, j: (l, 0, j)),
                  pl.BlockSpec((None, 1, tn), lambda l, j: (l, 0, j))],
        out_specs=pl.BlockSpec((None, SUBLANES, tn), lambda l, j: (l, 0, j)),
        out_shape=jax.ShapeDtypeStruct((nl, SUBLANES, n), F32),
        compiler_params=_params(2),
        name="mods",
    )(c_pad, w, b)


def _causal_conv(u, carry, cols, cw, cb):
    tm, n = u.shape
    width = cw.shape[0]
    groups = tm // SUBLANES
    tail = carry[:, cols]
    carry[:, cols] = u[tm - SUBLANES:, :]
    u3 = jnp.concatenate([tail, u], axis=0).reshape(groups + 1, SUBLANES, n)
    prev, cur = u3[:groups], u3[1:]
    row = lax.broadcasted_iota(jnp.int32, (groups, SUBLANES, n), 1)
    acc = cb + cw[width - 1:width, :] * u
    for shift in range(1, width):
        mixed = jnp.where(row >= SUBLANES - shift, prev, cur)
        shifted = pltpu.roll(mixed, shift, axis=1).reshape(tm, n)
        acc = acc + cw[width - 1 - shift:width - shift, :] * shifted
    return acc


def _ssm_in_kernel(x_ref, g_ref, sh_ref, sc_ref, wz_ref, wx_ref, wdt_hi_ref, wdt_lo_ref,
                   cw_ref, cb_ref, dtb_ref, z_ref, xbc_ref, dt_ref, carry):
    @pl.when(pl.program_id(1) == 0)
    def _():
        carry[...] = jnp.zeros(carry.shape, F32)

    hf = _rms_mod(x_ref[...], g_ref[...], sh_ref[...], sc_ref[...])
    h_hi, h_lo = _split2(hf)
    z_ref[...] = _dot(h_hi, wz_ref[...]).astype(BF16)
    dt_raw = (_dot(h_hi, wdt_hi_ref[...]) + _dot(h_hi, wdt_lo_ref[...])
              + _dot(h_lo, wdt_hi_ref[...]))
    dt_ref[...] = _softplus(dt_raw + dtb_ref[...])
    for c in range(SSM_CONV_DIM // SSM_COL_CHUNK):
        cols = slice(c * SSM_COL_CHUNK, (c + 1) * SSM_COL_CHUNK)
        u = _dot(h_hi, wx_ref[:, cols])
        v = _causal_conv(u, carry, cols, cw_ref[:, cols], cb_ref[:, cols])
        xbc_ref[:, cols] = _silu(v).astype(BF16)


def _ssm_in(x, g, sh, sc, wz, wx, wdt_hi, wdt_lo, cw, cb, dtb, tm):
    bsz, s, d = x.shape
    row = lambda n: pl.BlockSpec((None, tm, n), lambda b, j: (b, j, 0))
    vec = lambda n: pl.BlockSpec((None, 1, n), lambda b, j: (b, 0, 0))
    return pl.pallas_call(
        _ssm_in_kernel,
        grid=(bsz, s // tm),
        in_specs=[row(d), _const_spec((1, d)), vec(d), vec(d),
                  _const_spec(wz.shape), _const_spec(wx.shape),
                  _const_spec(wdt_hi.shape), _const_spec(wdt_lo.shape),
                  _const_spec(cw.shape), _const_spec(cb.shape), _const_spec(dtb.shape)],
        out_specs=[row(SSM_D_INNER), row(SSM_CONV_DIM), row(LANES)],
        out_shape=[jax.ShapeDtypeStruct((bsz, s, SSM_D_INNER), BF16),
                   jax.ShapeDtypeStruct((bsz, s, SSM_CONV_DIM), BF16),
                   jax.ShapeDtypeStruct((bsz, s, LANES), F32)],
        scratch_shapes=[pltpu.VMEM((SUBLANES, SSM_CONV_DIM), F32)],
        compiler_params=_params(2),
        name="ssm_in",
    )(x, g, sh, sc, wz, wx, wdt_hi, wdt_lo, cw, cb, dtb)


def _ssd_kernel(z_ref, xs_ref, b_ref, c_ref, dt_ref, x_ref, g1_ref, alog_ref, dskip_ref,
                ng_ref, wout_ref, xo_ref, state_ref, y_ref, *, tm):
    q = SSM_CHUNK

    @pl.when(pl.program_id(1) == 0)
    def _():
        state_ref[...] = jnp.zeros(state_ref.shape, F32)

    a_neg = -jnp.exp(alog_ref[...])
    ri = lax.broadcasted_iota(jnp.int32, (q, q), 0)
    ci = lax.broadcasted_iota(jnp.int32, (q, q), 1)
    causal = ri >= ci
    tril = jnp.where(causal, 1.0, 0.0).astype(BF16)
    first_head = ci < SSM_HEAD_DIM

    def chunk(ck, carry_unused):
        rows = pl.ds(pl.multiple_of(ck * q, q), q)
        dtc = dt_ref[rows, :]
        a3 = jnp.concatenate(_split3(dtc * a_neg), axis=1)
        r = _dot(tril, a3)
        acs = r[:, 0:LANES] + r[:, LANES:2 * LANES] + r[:, 2 * LANES:3 * LANES]
        acs_t = acs.T
        dt_t = dtc.T
        for grp in range(SSM_N_GROUPS):
            gcols = slice(grp * SSM_D_STATE, (grp + 1) * SSM_D_STATE)
            bg = b_ref[rows, gcols]
            cg = c_ref[rows, gcols]
            cb = _dot_nt(cg, bg)
            bg_t = bg.astype(F32).T
            cg_f = cg.astype(F32)
            for pr in range(SSM_HEADS_PER_GROUP // 2):
                h0 = grp * SSM_HEADS_PER_GROUP + 2 * pr
                pcols = slice(h0 * SSM_HEAD_DIM, (h0 + 2) * SSM_HEAD_DIM)
                xs_p = xs_ref[rows, pcols]
                st_p = state_ref[:, pcols]
                rhs = jnp.concatenate([xs_p, st_p.astype(BF16)], axis=0)
                outs, sts, cds = [], [], []
                for h in (h0, h0 + 1):
                    colb = jnp.broadcast_to(acs[:, h:h + 1], (q, q))
                    rowb = acs_t[h:h + 1, :]
                    dtrow = dt_t[h:h + 1, :]
                    seg = jnp.exp(jnp.where(causal, colb - rowb, -jnp.inf))
                    m = cb * seg * dtrow
                    ce = cg_f * jnp.exp(colb)
                    lhs = jnp.concatenate([m, ce], axis=1).astype(BF16)
                    outs.append(_dot(lhs, rhs))
                    last = colb[q - 1:q, :]
                    wrow = jnp.exp(last - rowb) * dtrow
                    sts.append(_dot((bg_t * wrow).astype(BF16), xs_p))
                    cds.append(jnp.exp(last))
                y_p = (jnp.where(first_head, outs[0], outs[1])
                       + xs_p.astype(F32) * dskip_ref[:, pcols])
                y_ref[rows, pcols] = y_p
                state_ref[:, pcols] = (st_p * jnp.where(first_head[0:1, :], cds[0], cds[1])
                                       + jnp.where(first_head, sts[0], sts[1]))
        return carry_unused

    lax.fori_loop(0, tm // q, chunk, 0)

    zf = z_ref[...].astype(F32)
    gy = y_ref[...] * _silu(zf)
    r = lax.rsqrt(jnp.mean(gy * gy, axis=-1, keepdims=True) + NORM_EPS)
    yn = (gy * r * ng_ref[...]).astype(BF16)
    xo_ref[...] = x_ref[...] + g1_ref[...] * _dot(yn, wout_ref[...])


def _ssd(z, xbc, dt, x, g1, alog, dskip, ng, wout, tm):
    bsz, s, d = x.shape
    row = lambda n, cb=0: pl.BlockSpec((None, tm, n), lambda b, j: (b, j, cb))
    vec = lambda n: pl.BlockSpec((None, 1, n), lambda b, j: (b, 0, 0))
    return pl.pallas_call(
        functools.partial(_ssd_kernel, tm=tm),
        grid=(bsz, s // tm),
        in_specs=[row(SSM_D_INNER), row(SSM_D_INNER, 0), row(SSM_GN, 2), row(SSM_GN, 3),
                  row(LANES), row(d), vec(d), _const_spec(alog.shape),
                  _const_spec(dskip.shape), _const_spec(ng.shape), _const_spec(wout.shape)],
        out_specs=row(d),
        out_shape=jax.ShapeDtypeStruct((bsz, s, d), F32),
        scratch_shapes=[pltpu.VMEM((SSM_D_STATE, SSM_D_INNER), F32),
                        pltpu.VMEM((tm, SSM_D_INNER), F32)],
        compiler_params=_params(2),
        name="ssd",
    )(z, xbc, xbc, xbc, dt, x, g1, alog, dskip, ng, wout)


def _ffn_kernel(*refs, tm, attn_in, qkv_out, final_out):
    it = iter(refs)
    x_ref = next(it)
    if attn_in:
        o_ref, wo_ref, g1_ref = next(it), next(it), next(it)
    ng_ref, sh_ref, sc_ref, g2_ref = next(it), next(it), next(it), next(it)
    wg_ref, wv_ref, cw_ref, cb_ref, wout_ref = next(it), next(it), next(it), next(it), next(it)
    if qkv_out:
        kvg_ref, kvsh_ref, kvsc_ref, wk_ref, wvv_ref = (next(it) for _ in range(5))
        qg_ref, qsh_ref, qsc_ref, wq_ref = (next(it) for _ in range(4))
    if final_out:
        fg_ref, fsh_ref, fsc_ref = next(it), next(it), next(it)
    xo_ref = next(it)
    if qkv_out:
        q_ref, k_ref, v_ref = next(it), next(it), next(it)
    carry, act = next(it), next(it)

    @pl.when(pl.program_id(1) == 0)
    def _():
        carry[...] = jnp.zeros(carry.shape, F32)

    x = x_ref[...]
    if attn_in:
        x = x + g1_ref[...] * _dot(o_ref[...], wo_ref[...])
    h = _rms_mod(x, ng_ref[...], sh_ref[...], sc_ref[...]).astype(BF16)
    fc = FFN_COL_CHUNK
    for c in range(FFN_DIM // fc):
        gcols = slice(c * fc, (c + 1) * fc)
        vcols = slice(FFN_DIM + c * fc, FFN_DIM + (c + 1) * fc)
        ug = _causal_conv(_dot(h, wg_ref[:, gcols]), carry, gcols, cw_ref[:, gcols], cb_ref[:, gcols])
        uv = _causal_conv(_dot(h, wv_ref[:, gcols]), carry, vcols, cw_ref[:, vcols], cb_ref[:, vcols])
        act[:, gcols] = (_silu(ug) * uv).astype(BF16)
    xn = x + g2_ref[...] * _dot(act[...], wout_ref[...])
    if final_out:
        xo_ref[...] = _rms_mod(xn, fg_ref[...], fsh_ref[...], fsc_ref[...])
    else:
        xo_ref[...] = xn
    if qkv_out:
        hk = _rms_mod(xn, kvg_ref[...], kvsh_ref[...], kvsc_ref[...]).astype(BF16)
        k_ref[...] = _dot(hk, wk_ref[...]).astype(BF16)
        v_ref[...] = _dot(hk, wvv_ref[...]).astype(BF16)
        hq = _rms_mod(xn, qg_ref[...], qsh_ref[...], qsc_ref[...]).astype(BF16)
        q_ref[...] = (_dot(hq, wq_ref[...]) * SB_Q_SCALE).astype(BF16)


def _ffn(x, norm, ffn_w, tm, attn=None, qkv=None, final=None):
    bsz, s, d = x.shape
    row = lambda n: pl.BlockSpec((None, tm, n), lambda b, j: (b, j, 0))
    vec = lambda n: pl.BlockSpec((None, 1, n), lambda b, j: (b, 0, 0))
    g, sh, sc, g2 = norm
    args, specs = [x], [row(d)]
    if attn is not None:
        o, wo, g1 = attn
        args += [o, wo, g1]
        specs += [row(d), _const_spec(wo.shape), vec(d)]
    args += [g, sh, sc, g2]
    specs += [_const_spec(g.shape), vec(d), vec(d), vec(d)]
    args += list(ffn_w)
    specs += [_const_spec(w.shape) for w in ffn_w]
    out_shape = [jax.ShapeDtypeStruct((bsz, s, d), F32)]
    out_specs = [row(d)]
    if qkv is not None:
        kvg, kvsh, kvsc, wk, wv, qg, qsh, qsc, wq = qkv
        args += [kvg, kvsh, kvsc, wk, wv, qg, qsh, qsc, wq]
        specs += [_const_spec(kvg.shape), vec(d), vec(d), _const_spec(wk.shape),
                  _const_spec(wv.shape), _const_spec(qg.shape), vec(d), vec(d),
                  _const_spec(wq.shape)]
        out_shape += [jax.ShapeDtypeStruct((bsz, s, d), BF16)] * 3
        out_specs += [row(d)] * 3
    if final is not None:
        fg, fsh, fsc = final
        args += [fg, fsh, fsc]
        specs += [_const_spec(fg.shape), vec(d), vec(d)]
    return pl.pallas_call(
        functools.partial(_ffn_kernel, tm=tm, attn_in=attn is not None,
                          qkv_out=qkv is not None, final_out=final is not None),
        grid=(bsz, s // tm),
        in_specs=specs,
        out_specs=out_specs,
        out_shape=out_shape,
        scratch_shapes=[pltpu.VMEM((SUBLANES, 2 * FFN_DIM), F32),
                        pltpu.VMEM((tm, FFN_DIM), BF16)],
        compiler_params=_params(2),
        name="ffn_qkv" if qkv is not None else "ffn_final",
    )(*args)


def _sb_kernel(q_ref, k_ref, v_ref, u2_ref, o_ref, *, seq):
    blk = SB_BLOCK
    tq = SB_Q_TILE
    key = lax.broadcasted_iota(jnp.int32, (tq, 2 * blk), 1) & (blk - 1)
    qrow = lax.broadcasted_iota(jnp.int32, (tq, 2 * blk), 0)
    live = key < qrow
    first_head = lax.broadcasted_iota(jnp.int32, (blk, blk), 1) < SB_HEAD_DIM
    u2 = u2_ref[...]

    def block_diag(t):
        zero = jnp.zeros_like(t)
        return jnp.concatenate([jnp.where(first_head, t, zero), jnp.where(first_head, zero, t)],
                               axis=0)

    def step(q, cols, kb, suffix, mask):
        rows = pl.ds(pl.multiple_of(kb * blk, blk), blk)
        z = _dot_nt(q, block_diag(k_ref[rows, cols]))
        log_b = jnp.minimum(z, 0.0) - jnp.log2(1.0 + jnp.exp2(-jnp.abs(z)))
        log_1mb = log_b - z
        if mask is not None:
            log_1mb = jnp.where(mask, log_1mb, 0.0)
        hi, lo = _split2(log_1mb)
        ra = _dot(jnp.concatenate([hi[:, :blk], lo[:, :blk]], axis=1), u2)
        rb = _dot(jnp.concatenate([hi[:, blk:], lo[:, blk:]], axis=1), u2)
        later = jnp.concatenate([ra[:, :blk], rb[:, :blk]], axis=1)
        row_total = jnp.concatenate([ra[:, blk:], rb[:, blk:]], axis=1)
        w = jnp.exp2(log_b + later + suffix)
        if mask is not None:
            w = jnp.where(mask, w, 0.0)
        return _dot(w.astype(BF16), block_diag(v_ref[rows, cols])), suffix + row_total

    def diag_tile(qi, q, cols):
        zeros = jnp.zeros((blk, 2 * blk), F32)
        pv_up, suf_up = step(q[blk:, :], cols, 2 * qi + 1, zeros, live[:blk, :])
        pv, suffix = step(q, cols, 2 * qi, jnp.concatenate([zeros, suf_up], axis=0), live)
        return pv + jnp.concatenate([jnp.zeros((blk, blk), F32), pv_up], axis=0), suffix

    def full_tile(kt, q, cols, acc, suffix):
        pv_a, suffix = step(q, cols, 2 * kt + 1, suffix, None)
        pv_b, suffix = step(q, cols, 2 * kt, suffix, None)
        return acc + pv_a + pv_b, suffix

    def alive(suffixes):
        top = functools.reduce(jnp.maximum, suffixes)
        return (jnp.max(top) > SB_LOG2_ZERO).astype(jnp.int32)

    pairs = [slice(p * LANES, (p + 1) * LANES) for p in range(q_ref.shape[-1] // LANES)]

    for cols in pairs:
        acc, _ = diag_tile(0, q_ref[0:tq, cols], cols)
        o_ref[0:tq, cols] = acc.astype(BF16)

    def q_tile(qi, carry_unused):
        rows = pl.ds(pl.multiple_of(qi * tq, tq), tq)
        qs = [q_ref[rows, cols] for cols in pairs]
        accs, sufs = [], []
        for q, cols in zip(qs, pairs):
            acc, suffix = diag_tile(qi, q, cols)
            acc, suffix = full_tile(qi - 1, q, cols, acc, suffix)
            accs.append(acc)
            sufs.append(suffix)

        def cond(state):
            return jnp.logical_and(state[0] >= 0, state[3] > 0)

        def body(state):
            kt, accs, sufs, _ = state
            out = [full_tile(kt, q, cols, acc, suffix)
                   for q, cols, acc, suffix in zip(qs, pairs, accs, sufs)]
            sufs = [o[1] for o in out]
            return kt - 1, [o[0] for o in out], sufs, alive(sufs)

        _, accs, _, _ = lax.while_loop(cond, body, (qi - 2, accs, sufs, alive(sufs)))
        for cols, acc in zip(pairs, accs):
            o_ref[rows, cols] = acc.astype(BF16)
        return carry_unused

    lax.fori_loop(1, seq // tq, q_tile, 0)


def _sb_attn(q, k, v, u2):
    bsz, s, d = q.shape
    width = SB_PAIRS_PER_STEP * LANES
    col = pl.BlockSpec((None, s, width), lambda b, p: (b, 0, p))
    return pl.pallas_call(
        functools.partial(_sb_kernel, seq=s),
        grid=(bsz, d // width),
        in_specs=[col, col, col, _const_spec(u2.shape)],
        out_specs=col,
        out_shape=jax.ShapeDtypeStruct((bsz, s, d), BF16),
        compiler_params=_params(2),
        name="sb_attn",
    )(q, k, v, u2)


def kernel(x, c, ada_w, ada_b, norm_mix_g, norm_ffn_g, ssm_w_in, ssm_conv_w, ssm_conv_b, ssm_dt_bias, ssm_a_log, ssm_d, ssm_norm_g, ssm_w_out, kv_ada_w, kv_ada_b, kv_norm_g, w_k, w_v, sb_w_q, sb_w_o, ffn_w_in, ffn_conv_w, ffn_conv_b, ffn_w_out, final_ada_w, final_ada_b, final_norm_g):
    bsz, s, d = x.shape
    tm = min(TOKEN_TILE, s)
    assert d == D_MODEL and s % tm == 0 and tm % SSM_CHUNK == 0 and bsz <= SUBLANES

    c_pad = jnp.zeros((SUBLANES, d), F32).at[:bsz].set(c)
    mod = _mods(c_pad, ada_w, ada_b[:, None, :])
    kv_mod = _mods(c_pad, kv_ada_w[None], kv_ada_b[None, None, :])[0]
    f_mod = _mods(c_pad, final_ada_w[None], final_ada_b[None, None, :])[0]
    part = lambda m, i: m[:bsz, None, i * d:(i + 1) * d]
    sh1a, sc1a, g1a, sh2a, sc2a, g2a = (part(mod[0], i) for i in range(6))
    sh1b, sc1b, g1b, sh2b, sc2b, g2b = (part(mod[1], i) for i in range(6))
    kv_sh, kv_sc = part(kv_mod, 0), part(kv_mod, 1)
    f_sh, f_sc = part(f_mod, 0), part(f_mod, 1)
    row = lambda a: a.reshape(1, -1)

    w_in = ssm_w_in[0]
    wz = w_in[:, :SSM_D_INNER].astype(BF16)
    wx = w_in[:, SSM_D_INNER:SSM_D_INNER + SSM_CONV_DIM].astype(BF16)
    pad_h = LANES - SSM_N_HEADS
    wdt = jnp.pad(w_in[:, SSM_D_INNER + SSM_CONV_DIM:], ((0, 0), (0, pad_h)))
    wdt_hi, wdt_lo = _split2(wdt)
    dtb = jnp.pad(ssm_dt_bias[0], (0, pad_h)).reshape(1, LANES)
    z, xbc, dt = _ssm_in(x, row(norm_mix_g[0]), sh1a, sc1a, wz, wx, wdt_hi, wdt_lo,
                         ssm_conv_w[0], row(ssm_conv_b[0]), dtb, tm)
    alog = jnp.pad(ssm_a_log[0], (0, pad_h)).reshape(1, LANES)
    dskip = jnp.repeat(ssm_d[0], SSM_HEAD_DIM).reshape(1, SSM_D_INNER)
    x = _ssd(z, xbc, dt, x, g1a, alog, dskip, row(ssm_norm_g[0]),
             ssm_w_out[0].astype(BF16), tm)

    def ffn_weights(layer):
        w = ffn_w_in[layer]
        return (w[:, :FFN_DIM].astype(BF16), w[:, FFN_DIM:].astype(BF16), ffn_conv_w[layer],
                row(ffn_conv_b[layer]), ffn_w_out[layer].astype(BF16))

    x, q, k, v = _ffn(x, (row(norm_ffn_g[0]), sh2a, sc2a, g2a), ffn_weights(0), tm,
                      qkv=(row(kv_norm_g), kv_sh, kv_sc, w_k.astype(BF16), w_v.astype(BF16),
                           row(norm_mix_g[1]), sh1b, sc1b, sb_w_q[0].astype(BF16)))

    ji = lax.broadcasted_iota(jnp.int32, (SB_BLOCK, 2 * SB_BLOCK), 0)
    si = lax.broadcasted_iota(jnp.int32, (SB_BLOCK, 2 * SB_BLOCK), 1)
    u1 = jnp.where(jnp.logical_or(si >= SB_BLOCK, ji > si), 1.0, 0.0).astype(BF16)
    u2 = jnp.concatenate([u1, u1], axis=0)
    o = _sb_attn(q, k, v, u2)
    return _ffn(x, (row(norm_ffn_g[1]), sh2b, sc2b, g2b), ffn_weights(1), tm,
                attn=(o, sb_w_o[0].astype(BF16), g1b),
                final=(row(final_norm_g), f_sh, f_sc))[0]
```
